```python
import jax, jax.numpy as jnp
from jax import lax
import numpy as np

D_MODEL = 2048
BATCH = 16
SEQ = 256
DEPTH = 2
DEC_BATCH = 4
DEC_SEQ = 1024
PAST_LEN = 512

GRID_W = 64
BLOCK = 128
EPS = 1e-6
ROPE_BASE = 10000.0
NEG_INF = -1e30

MLA_HEADS = 8
MLA_Q_RANK = 512
MLA_KV_RANK = 256
MLA_NOPE = 128
MLA_ROPE = 64
MLA_V = 128
MLA_QK = MLA_NOPE + MLA_ROPE
MLA_WIDTH = MLA_HEADS * MLA_V
CONV_CH = 512
CONV_W = 31
SWA_HEADS = 4
SWA_KV_HEADS = 2
SWA_HD = 128
SWA_WIN = 128
SWA_WIDTH = SWA_HEADS * SWA_HD
MIX_WIDTH = MLA_WIDTH + CONV_CH + SWA_WIDTH

SPLIT_SIZES = (MLA_Q_RANK, MLA_KV_RANK, MLA_ROPE, 2 * CONV_CH,
               SWA_HEADS * SWA_HD, SWA_KV_HEADS * SWA_HD, SWA_KV_HEADS * SWA_HD)
IN_COLS = sum(SPLIT_SIZES)
SPLIT_IDX = tuple(int(i) for i in np.cumsum(SPLIT_SIZES)[:-1])

PEER_HEADS = 8
PEER_NKEYS = 128
PEER_EXPERTS = PEER_NKEYS * PEER_NKEYS
PEER_DQ = 256
PEER_HALF = PEER_DQ // 2
PEER_TOPK = 16

kernel_name = 'hybrid_mla_conv_swa_peer_diffusion_step'


def rmsnorm(x, g):
    xf = x.astype(jnp.float32)
    y = xf * lax.rsqrt(jnp.mean(xf * xf, axis=-1, keepdims=True) + EPS)
    return (y * g.astype(jnp.float32)).astype(x.dtype)


def layernorm(x, g, b):
    xf = x.astype(jnp.float32)
    xc = xf - jnp.mean(xf, axis=-1, keepdims=True)
    y = xc * lax.rsqrt(jnp.mean(xc * xc, axis=-1, keepdims=True) + EPS)
    return (y * g.astype(jnp.float32) + b.astype(jnp.float32)).astype(x.dtype)


def axial_angles(n_rows, rot_dim):
    rows = jnp.repeat(jnp.arange(n_rows), GRID_W).astype(jnp.float32)
    cols = jnp.tile(jnp.arange(GRID_W), n_rows).astype(jnp.float32)
    a = rot_dim // 2
    freqs = ROPE_BASE ** (-jnp.arange(0, a, 2, dtype=jnp.float32) / a)
    return rows[:, None] * freqs, cols[:, None] * freqs


def rope_1d(x, ang):
    cos = jnp.cos(ang)[:, None, :].astype(x.dtype)
    sin = jnp.sin(ang)[:, None, :].astype(x.dtype)
    x1, x2 = jnp.split(x, 2, axis=-1)
    return jnp.concatenate([x1 * cos - x2 * sin, x1 * sin + x2 * cos], axis=-1)


def rope_2d(x, angs):
    xr, xc = jnp.split(x, 2, axis=-1)
    return jnp.concatenate([rope_1d(xr, angs[0]), rope_1d(xc, angs[1])], axis=-1)


def rope_tail(x, angs):
    return jnp.concatenate([x[..., :MLA_NOPE], rope_2d(x[..., MLA_NOPE:], angs)], axis=-1)


def modulation(cvec, p):
    m = jax.nn.silu(cvec) @ p['w_mod'] + p['b_mod']
    return jnp.split(m[:, None, :], 6, axis=-1)


def mixing_inputs(x, shift, scale, p):
    h = rmsnorm(x, p['norm1_g']) * (1 + scale) + shift
    return jnp.split(h @ p['w_in'], SPLIT_IDX, axis=-1)


def mla_queries(q_a, p):
    B, L, _ = q_a.shape
    q = (rmsnorm(q_a, p['q_a_norm_g']) @ p['w_q_b']).reshape(B, L, MLA_HEADS, MLA_QK)
    return rmsnorm(q, p['mla_q_norm_g'])


def mla_keys_values(ckv, k_rope, p):
    B, L, _ = ckv.shape
    kv = (ckv @ p['w_kv_b']).reshape(B, L, MLA_HEADS, MLA_NOPE + MLA_V)
    k_nope, v = jnp.split(kv, [MLA_NOPE], axis=-1)
    k_pe = jnp.broadcast_to(k_rope[:, :, None, :], (B, L, MLA_HEADS, MLA_ROPE))
    k = rmsnorm(jnp.concatenate([k_nope, k_pe], axis=-1), p['mla_k_norm_g'])
    return k, v


def conformer_conv(u, p):
    a, b = jnp.split(u, 2, axis=-1)
    g = a * jax.nn.sigmoid(b)
    y = lax.conv_general_dilated(
        g, p['conv_dw_w'][:, None, :], window_strides=(1,),
        padding=[(CONV_W // 2, CONV_W // 2)],
        dimension_numbers=('NWC', 'WIO', 'NWC'),
        feature_group_count=CONV_CH) + p['conv_dw_b']
    return jax.nn.silu(layernorm(y, p['conv_ln_g'], p['conv_ln_b']))


def swa_qkv(sq, sk, sv, p):
    B, L, _ = sq.shape
    q = rmsnorm(sq.reshape(B, L, SWA_HEADS, SWA_HD), p['swa_q_norm_g'])
    k = rmsnorm(sk.reshape(B, L, SWA_KV_HEADS, SWA_HD), p['swa_k_norm_g'])
    return q, k, sv.reshape(B, L, SWA_KV_HEADS, SWA_HD)


def sink_softmax(s, sink):
    Hk, G = s.shape[1], s.shape[2]
    s0 = jnp.broadcast_to(sink.astype(jnp.float32).reshape(Hk, G)[None, :, :, None, None],
                          s.shape[:-1] + (1,))
    return jax.nn.softmax(jnp.concatenate([s0, s], axis=-1), axis=-1)[..., 1:]


def attn_blocked(q, k, v, sink=None):
    B, Lq, H, d = q.shape
    Hk = k.shape[2]
    G = H // Hk
    dv = v.shape[-1]
    nb = Lq // BLOCK
    qb = jnp.moveaxis(q.reshape(B, nb, BLOCK, Hk, G, d), 1, 0)
    scale = d ** -0.5

    def one(qn):
        s = jnp.einsum('bqkgd,bskd->bkgqs', qn, k).astype(jnp.float32) * scale
        pr = jax.nn.softmax(s, axis=-1) if sink is None else sink_softmax(s, sink)
        o = jnp.einsum('bkgqs,bskd->bqkgd', pr.astype(v.dtype), v)
        return o.reshape(B, BLOCK, H, dv)

    out = lax.map(one, qb)
    return jnp.moveaxis(out, 0, 1).reshape(B, Lq, H, dv)


def swa_banded(q, k, v, kc, vc, sink):
    B, S, H, d = q.shape
    Hk = k.shape[2]
    G = H // Hk
    nb = S // BLOCK
    pad = ((0, 0), (BLOCK, BLOCK), (0, 0), (0, 0))
    kp = jnp.pad(k, pad).reshape(B, nb + 2, BLOCK, Hk, d)
    vp = jnp.pad(v, pad).reshape(B, nb + 2, BLOCK, Hk, d)
    kband = jnp.concatenate([kp[:, :-2], kp[:, 1:-1], kp[:, 2:]], axis=2)
    vband = jnp.concatenate([vp[:, :-2], vp[:, 1:-1], vp[:, 2:]], axis=2)
    qb = q.reshape(B, nb, BLOCK, Hk, G, d)
    qi = jnp.arange(BLOCK)[:, None]
    sj = jnp.arange(3 * BLOCK)[None, :]
    kpos = jnp.arange(nb)[:, None, None] * BLOCK + sj - BLOCK
    mask = (jnp.abs(sj - BLOCK - qi) <= SWA_WIN)[None] & (kpos >= 0) & (kpos < S)
    scale = d ** -0.5
    n_band = 3 * BLOCK

    def one(args):
        qn, kn, vn, mn = args
        s_band = jnp.einsum('bqkgd,bskd->bkgqs', qn, kn).astype(jnp.float32) * scale
        s_band = jnp.where(mn, s_band, NEG_INF)
        s_ctx = jnp.einsum('bqkgd,bckd->bkgqc', qn, kc).astype(jnp.float32) * scale
        pr = sink_softmax(jnp.concatenate([s_band, s_ctx], axis=-1), sink)
        o = (jnp.einsum('bkgqs,bskd->bqkgd', pr[..., :n_band].astype(vn.dtype), vn)
             + jnp.einsum('bkgqc,bckd->bqkgd', pr[..., n_band:].astype(vc.dtype), vc))
        return o.reshape(B, BLOCK, H, d)

    xs = (jnp.moveaxis(qb, 1, 0), jnp.moveaxis(kband, 1, 0), jnp.moveaxis(vband, 1, 0), mask)
    out = lax.map(one, xs)
    return jnp.moveaxis(out, 0, 1).reshape(B, S, H, d)


def peer(h, p):
    B, L, D = h.shape
    xs = h.reshape(-1, BLOCK, D)
    w_q, sub_keys, u, v = p['peer_w_q'], p['peer_keys'], p['peer_u'], p['peer_v']

    def one(xc):
        q = (xc @ w_q).reshape(BLOCK, PEER_HEADS, 2, PEER_HALF)
        s = jnp.einsum('thpd,hpnd->thpn', q, sub_keys).astype(jnp.float32)
        sv, si = lax.top_k(s, PEER_TOPK)
        cand = (sv[:, :, 0, :, None] + sv[:, :, 1, None, :]).reshape(BLOCK, PEER_HEADS, PEER_TOPK * PEER_TOPK)
        cv, ci = lax.top_k(cand, PEER_TOPK)
        e = (jnp.take_along_axis(si[:, :, 0], ci // PEER_TOPK, axis=-1) * PEER_NKEYS
             + jnp.take_along_axis(si[:, :, 1], ci % PEER_TOPK, axis=-1))
        g = jax.nn.softmax(cv, axis=-1)
        a = jnp.einsum('thkd,td->thk', u[e], xc).astype(jnp.float32)
        w = (g * jax.nn.gelu(a, approximate=False)).astype(xc.dtype)
        return jnp.einsum('thk,thkd->td', w, v[e])

    return lax.map(one, xs).reshape(B, L, D)


def mix_output(o_mla, o_conv, o_swa, p):
    B, L, _ = o_conv.shape
    o = jnp.concatenate([o_mla.reshape(B, L, MLA_WIDTH), o_conv, o_swa.reshape(B, L, SWA_WIDTH)], axis=-1)
    return o @ p['w_out']


def channel_sublayer(x, shift, scale, gate, p):
    return x + gate * peer(rmsnorm(x, p['norm2_g']) * (1 + scale) + shift, p)


def context_layer(x, c_ctx, p):
    sh1, sc1, g1, sh2, sc2, g2 = modulation(c_ctx[None, :], p)
    q_a, kv_a, k_rope, conv_in, sq, sk, sv = mixing_inputs(x, sh1, sc1, p)
    q = mla_queries(q_a, p)
    ckv = rmsnorm(kv_a, p['kv_a_norm_g'])
    k, v = mla_keys_values(ckv, k_rope, p)
    o_mla = attn_blocked(q, k, v)
    o_conv = conformer_conv(conv_in, p)
    q2, k2, v2 = swa_qkv(sq, sk, sv, p)
    o_swa = attn_blocked(q2, k2, v2, sink=p['swa_sink'])
    x = x + g1 * mix_output(o_mla, o_conv, o_swa, p)
    x = channel_sublayer(x, sh2, sc2, g2, p)
    return x, (ckv, k_rope, k2, v2)


def latent_layer(x, c, ckv_c, krope_c, k_c, v_c, p, ang_mla, ang_swa):
    sh1, sc1, g1, sh2, sc2, g2 = modulation(c, p)
    q_a, kv_a, k_rope, conv_in, sq, sk, sv = mixing_inputs(x, sh1, sc1, p)
    q = rope_tail(mla_queries(q_a, p), ang_mla)
    k, v = mla_keys_values(rmsnorm(kv_a, p['kv_a_norm_g']), k_rope, p)
    k = rope_tail(k, ang_mla)
    kc, vc = mla_keys_values(ckv_c, krope_c, p)
    o_mla = attn_blocked(q, jnp.concatenate([k, kc], axis=1), jnp.concatenate([v, vc], axis=1))
    o_conv = conformer_conv(conv_in, p)
    q2, k2, v2 = swa_qkv(sq, sk, sv, p)
    o_swa = swa_banded(rope_2d(q2, ang_swa), rope_2d(k2, ang_swa), v2, k_c, v_c, p['swa_sink'])
    x = x + g1 * mix_output(o_mla, o_conv, o_swa, p)
    return channel_sublayer(x, sh2, sc2, g2, p)


def setup_inputs(seed: int = 0) -> dict:
    key = jax.random.key(seed)
    ks = iter(list(jax.random.split(key, 40)))
    nrm = lambda shape, s: jax.random.normal(next(ks), shape, jnp.float32) * s
    gain = lambda shape: 1.0 + 0.02 * jax.random.normal(next(ks), shape, jnp.float32)
    return {
        'x_prompt': nrm((BATCH, SEQ, D_MODEL), 1.0),
        'x_sample': nrm((DEC_BATCH, DEC_SEQ, D_MODEL), 1.0),
        'cache_mla_ckv': nrm((DEC_BATCH, DEPTH, PAST_LEN, MLA_KV_RANK), 1.0),
        'cache_mla_krope': nrm((DEC_BATCH, DEPTH, PAST_LEN, MLA_ROPE), 1.0),
        'cache_swa_k': nrm((DEC_BATCH, DEPTH, PAST_LEN, SWA_KV_HEADS, SWA_HD), 1.0),
        'cache_swa_v': nrm((DEC_BATCH, DEPTH, PAST_LEN, SWA_KV_HEADS, SWA_HD), 1.0),
        'c': nrm((DEC_BATCH, D_MODEL), 1.0),
        'c_ctx': nrm((D_MODEL,), 1.0),
        'norm1_g': gain((DEPTH, D_MODEL)),
        'w_mod': nrm((DEPTH, D_MODEL, 6 * D_MODEL), 0.5 * D_MODEL ** -0.5),
        'b_mod': nrm((DEPTH, 6 * D_MODEL), 0.02),
        'w_in': nrm((DEPTH, D_MODEL, IN_COLS), D_MODEL ** -0.5),
        'q_a_norm_g': gain((DEPTH, MLA_Q_RANK)),
        'w_q_b': nrm((DEPTH, MLA_Q_RANK, MLA_HEADS * MLA_QK), MLA_Q_RANK ** -0.5),
        'kv_a_norm_g': gain((DEPTH, MLA_KV_RANK)),
        'w_kv_b': nrm((DEPTH, MLA_KV_RANK, MLA_HEADS * (MLA_NOPE + MLA_V)), MLA_KV_RANK ** -0.5),
        'mla_q_norm_g': gain((DEPTH, MLA_QK)),
        'mla_k_norm_g': gain((DEPTH, MLA_QK)),
        'conv_dw_w': nrm((DEPTH, CONV_W, CONV_CH), CONV_W ** -0.5),
        'conv_dw_b': nrm((DEPTH, CONV_CH), 0.02),
        'conv_ln_g': gain((DEPTH, CONV_CH)),
        'conv_ln_b': nrm((DEPTH, CONV_CH), 0.02),
        'swa_q_norm_g': gain((DEPTH, SWA_HD)),
        'swa_k_norm_g': gain((DEPTH, SWA_HD)),
        'swa_sink': nrm((DEPTH, SWA_HEADS), 0.5),
        'w_out': nrm((DEPTH, MIX_WIDTH, D_MODEL), MIX_WIDTH ** -0.5),
        'norm2_g': gain((DEPTH, D_MODEL)),
        'peer_w_q': nrm((DEPTH, D_MODEL, PEER_HEADS * PEER_DQ), D_MODEL ** -0.5),
        'peer_keys': nrm((DEPTH, PEER_HEADS, 2, PEER_NKEYS, PEER_HALF), PEER_HALF ** -0.5),
        'peer_u': nrm((DEPTH, PEER_EXPERTS, D_MODEL), D_MODEL ** -0.5),
        'peer_v': nrm((DEPTH, PEER_EXPERTS, D_MODEL), 1.0),
    }


def reference(x_prompt, x_sample, cache_mla_ckv, cache_mla_krope, cache_swa_k, cache_swa_v,
              c, c_ctx, norm1_g, w_mod, b_mod, w_in, q_a_norm_g, w_q_b, kv_a_norm_g, w_kv_b,
              mla_q_norm_g, mla_k_norm_g, conv_dw_w, conv_dw_b, conv_ln_g, conv_ln_b,
              swa_q_norm_g, swa_k_norm_g, swa_sink, w_out, norm2_g,
              peer_w_q, peer_keys, peer_u, peer_v):
    S = x_sample.shape[1]
    ROWS = S // GRID_W
    ang_mla = axial_angles(ROWS, MLA_ROPE)
    ang_swa = axial_angles(ROWS, SWA_HD)
    yp = x_prompt
    ys = x_sample
    ckv_l, krope_l, k_l, v_l = [], [], [], []
    for l in range(DEPTH):
        p = {
            'norm1_g': norm1_g[l], 'w_mod': w_mod[l], 'b_mod': b_mod[l], 'w_in': w_in[l],
            'q_a_norm_g': q_a_norm_g[l], 'w_q_b': w_q_b[l], 'kv_a_norm_g': kv_a_norm_g[l],
            'w_kv_b': w_kv_b[l], 'mla_q_norm_g': mla_q_norm_g[l], 'mla_k_norm_g': mla_k_norm_g[l],
            'conv_dw_w': conv_dw_w[l], 'conv_dw_b': conv_dw_b[l], 'conv_ln_g': conv_ln_g[l],
            'conv_ln_b': conv_ln_b[l], 'swa_q_norm_g': swa_q_norm_g[l], 'swa_k_norm_g': swa_k_norm_g[l],
            'swa_sink': swa_sink[l], 'w_out': w_out[l], 'norm2_g': norm2_g[l],
            'peer_w_q': peer_w_q[l], 'peer_keys': peer_keys[l], 'peer_u': peer_u[l], 'peer_v': peer_v[l],
        }
        yp, (ckv, krope, kk, vv) = context_layer(yp, c_ctx, p)
        ckv_l.append(ckv)
        krope_l.append(krope)
        k_l.append(kk)
        v_l.append(vv)
        ys = latent_layer(ys, c, cache_mla_ckv[:, l], cache_mla_krope[:, l],
                          cache_swa_k[:, l], cache_swa_v[:, l], p, ang_mla, ang_swa)
    new_mla_ckv = jnp.stack(ckv_l, axis=1)
    new_mla_krope = jnp.stack(krope_l, axis=1)
    new_swa_k = jnp.stack(k_l, axis=1)
    new_swa_v = jnp.stack(v_l, axis=1)
    return (yp, ys, new_mla_ckv, new_mla_krope, new_swa_k, new_swa_v)
```

```python
import functools

import jax
import jax.numpy as jnp
from jax import lax
from jax.experimental import pallas as pl
from jax.experimental.pallas import tpu as pltpu

D_MODEL = 2048
BATCH = 16
SEQ = 256
DEPTH = 2
DEC_BATCH = 4
DEC_SEQ = 1024
PAST_LEN = 512
GRID_W = 64
EPS = 1e-6
ROPE_BASE = 10000.0
NEG_INF = -1e30

MLA_HEADS = 8
MLA_Q_RANK = 512
MLA_KV_RANK = 256
MLA_NOPE = 128
MLA_ROPE = 64
MLA_V = 128
MLA_QK = MLA_NOPE + MLA_ROPE
MLA_WIDTH = MLA_HEADS * MLA_V
MLA_HP = 256
CONV_CH = 512
CONV_W = 31
SWA_HEADS = 4
SWA_KV_HEADS = 2
SWA_GROUP = SWA_HEADS // SWA_KV_HEADS
SWA_HD = 128
SWA_WIN = 128
SWA_WIDTH = SWA_HEADS * SWA_HD
PEER_HEADS = 8
PEER_NKEYS = 128
PEER_EXPERTS = PEER_NKEYS * PEER_NKEYS
PEER_DQ = 256
PEER_HALF = PEER_DQ // 2
PEER_TOPK = 16

N_CTX = BATCH * SEQ
N_LAT = DEC_BATCH * DEC_SEQ
N_TOK = N_CTX + N_LAT
N_CACHE = DEC_BATCH * PAST_LEN
MOD_ROWS = 8

LANES = 128
VMEM_LIMIT = 56 * 1024 * 1024

C_QA = 0
C_KVA = 512
C_KR = 768
C_CONV = 896
C_SQ = 1920
C_SK = 2432
C_SV = 2688
IN_COLS_P = 2944

F32 = jnp.float32
BF16 = jnp.bfloat16
BIG_NEG = -3.0e38

_NT = (((1,), (1,)), ((), ()))


def _cparams(sem):
    return pltpu.CompilerParams(dimension_semantics=sem, vmem_limit_bytes=VMEM_LIMIT)


def _const_spec(shape):
    nd = len(shape)
    return pl.BlockSpec(shape, lambda *_: (0,) * nd, pipeline_mode=pl.Buffered(1))


def _group_of_tile(i, tile):
    first_lat = N_CTX // tile
    per_batch = DEC_SEQ // tile
    return jnp.where(i < first_lat, 0, 1 + (i - first_lat) // per_batch)


def _mod_spec(layer, chunk, tile):
    return pl.BlockSpec((None, None, 1, D_MODEL),
                        lambda i, *_: (layer, _group_of_tile(i, tile), 0, chunk))


def _rms(x, width):
    return lax.rsqrt(jnp.sum(x * x, axis=-1, keepdims=True) * (1.0 / width) + EPS)


MOD_TN = 1024


def _mod_kernel(c_ref, w_ref, b_ref, o_ref):
    cv = c_ref[...]
    s = cv * jax.nn.sigmoid(cv)
    o_ref[...] = jnp.dot(s.astype(BF16), w_ref[...].astype(BF16), preferred_element_type=F32) + b_ref[...]


def _modulation(cvec, w_mod, b_mod):
    n = 6 * D_MODEL
    return pl.pallas_call(
        _mod_kernel,
        grid=(DEPTH, n // MOD_TN),
        in_specs=[
            pl.BlockSpec((MOD_ROWS, D_MODEL), lambda l, j: (0, 0)),
            pl.BlockSpec((None, D_MODEL, MOD_TN), lambda l, j: (l, 0, j)),
            pl.BlockSpec((None, 1, MOD_TN), lambda l, j: (l, 0, j)),
        ],
        out_specs=pl.BlockSpec((None, MOD_ROWS, MOD_TN), lambda l, j: (l, 0, j)),
        out_shape=jax.ShapeDtypeStruct((DEPTH, MOD_ROWS, n), F32),
        compiler_params=_cparams(("arbitrary", "arbitrary")),
        name="modulation",
    )(cvec, w_mod, b_mod.reshape(DEPTH, 1, n))


PROJ_TM = 256


def _swap_groups(x, gs):
    w = x.shape[-1]
    lane = lax.broadcasted_iota(jnp.int32, x.shape, x.ndim - 1)
    even = (lane // gs) % 2 == 0
    return jnp.where(even, pltpu.roll(x, w - gs, x.ndim - 1), pltpu.roll(x, gs, x.ndim - 1))


def _rope(x, cos, sin_signed, gs):
    return x * cos + _swap_groups(x, gs) * sin_signed


def _mla_keys(ckv_bf, kpe, w_kvb_ref, gk_ref, rope_tables, km_ref, vm_ref):
    kv = jnp.dot(ckv_bf, w_kvb_ref[...], preferred_element_type=F32)
    vm_ref[...] = kv[:, MLA_HEADS * MLA_NOPE:].astype(BF16)
    gk = gk_ref[...]
    kpe_ss = jnp.sum(kpe * kpe, axis=-1, keepdims=True)
    kpe_g = kpe * gk[:, MLA_NOPE:]
    if rope_tables is not None:
        kpe_g = _rope(kpe_g, rope_tables[0], rope_tables[1], MLA_ROPE // 4)
    for hh in range(MLA_HEADS):
        kn = kv[:, hh * MLA_NOPE:(hh + 1) * MLA_NOPE]
        r = lax.rsqrt((jnp.sum(kn * kn, axis=-1, keepdims=True) + kpe_ss) * (1.0 / MLA_QK) + EPS)
        km_ref[:, hh * MLA_HP:hh * MLA_HP + MLA_NOPE] = (kn * r * gk[:, :MLA_NOPE]).astype(BF16)
        km_ref[:, hh * MLA_HP + MLA_NOPE:(hh + 1) * MLA_HP] = (kpe_g * r).astype(BF16)


def _proj_kernel(x_ref, sh_ref, sc_ref, g1_ref, w_in_ref, gqa_ref, w_qb_ref, gq_ref, gkv_ref, w_kvb_ref,
                 gk_ref, gsq_ref, gsk_ref, cm_ref, sm_ref, cs_ref, ss_ref,
                 qm_ref, km_ref, vm_ref, gg_ref, qs_ref, ks_ref, vs_ref, ckv_ref, kr_ref, k2_ref, v2_ref):
    x = x_ref[...]
    h = x * _rms(x, D_MODEL) * g1_ref[...]
    h = h * (1.0 + sc_ref[...]) + sh_ref[...]
    y = jnp.dot(h.astype(BF16), w_in_ref[...], preferred_element_type=F32)
    cm, sm = cm_ref[...], sm_ref[...]
    cs, ss = cs_ref[...], ss_ref[...]

    qa = y[:, C_QA:C_QA + MLA_Q_RANK]
    qn = qa * _rms(qa, MLA_Q_RANK) * gqa_ref[...]
    q = jnp.dot(qn.astype(BF16), w_qb_ref[...], preferred_element_type=F32)
    gq = gq_ref[...]
    for hh in range(MLA_HEADS):
        qh = q[:, hh * MLA_HP:(hh + 1) * MLA_HP]
        qh = qh * _rms(qh, MLA_QK) * gq
        qr = _rope(qh[:, MLA_NOPE:], cm, sm, MLA_ROPE // 4)
        qm_ref[:, hh * MLA_HP:hh * MLA_HP + MLA_NOPE] = qh[:, :MLA_NOPE].astype(BF16)
        qm_ref[:, hh * MLA_HP + MLA_NOPE:(hh + 1) * MLA_HP] = qr.astype(BF16)

    kva = y[:, C_KVA:C_KVA + MLA_KV_RANK]
    ckv = kva * _rms(kva, MLA_KV_RANK) * gkv_ref[...]
    ckv_ref[...] = ckv
    kpe = y[:, C_KR:C_KR + LANES]
    kr_ref[...] = kpe[:, :MLA_ROPE]
    _mla_keys(ckv.astype(BF16), kpe, w_kvb_ref, gk_ref, (cm, sm), km_ref, vm_ref)

    a = y[:, C_CONV:C_CONV + CONV_CH]
    b = y[:, C_CONV + CONV_CH:C_CONV + 2 * CONV_CH]
    gg_ref[...] = a * jax.nn.sigmoid(b)

    gsq, gsk = gsq_ref[...], gsk_ref[...]
    for hh in range(SWA_HEADS):
        sq = y[:, C_SQ + hh * SWA_HD:C_SQ + (hh + 1) * SWA_HD]
        sq = sq * _rms(sq, SWA_HD) * gsq
        qs_ref[:, hh * SWA_HD:(hh + 1) * SWA_HD] = _rope(sq, cs, ss, SWA_HD // 4).astype(BF16)
    for hh in range(SWA_KV_HEADS):
        sk = y[:, C_SK + hh * SWA_HD:C_SK + (hh + 1) * SWA_HD]
        sk = sk * _rms(sk, SWA_HD) * gsk
        k2_ref[:, hh * SWA_HD:(hh + 1) * SWA_HD] = sk
        ks_ref[:, hh * SWA_HD:(hh + 1) * SWA_HD] = _rope(sk, cs, ss, SWA_HD // 4).astype(BF16)
    sv = y[:, C_SV:C_SV + SWA_KV_HEADS * SWA_HD]
    v2_ref[...] = sv
    vs_ref[...] = sv.astype(BF16)


def _project(layer, x, mod, g1, w_in_p, gqa, w_qb_p, gq, gkv, w_kvb_p, gk, gsq, gsk, tables):
    tm = PROJ_TM
    row = lambda w: pl.BlockSpec((tm, w), lambda i: (i, 0))
    vec = lambda w: _const_spec((1, w))
    outs = [(2048, BF16), (2048, BF16), (MLA_WIDTH, BF16), (CONV_CH, F32), (SWA_WIDTH, BF16),
            (SWA_KV_HEADS * SWA_HD, BF16), (SWA_KV_HEADS * SWA_HD, BF16),
            (MLA_KV_RANK, F32), (MLA_ROPE, F32), (SWA_KV_HEADS * SWA_HD, F32), (SWA_KV_HEADS * SWA_HD, F32)]
    return pl.pallas_call(
        _proj_kernel,
        grid=(N_TOK // tm,),
        in_specs=[row(D_MODEL), _mod_spec(layer, 0, tm), _mod_spec(layer, 1, tm), vec(D_MODEL),
                  _const_spec(w_in_p.shape), vec(MLA_Q_RANK), _const_spec(w_qb_p.shape), vec(MLA_HP),
                  vec(MLA_KV_RANK), _const_spec(w_kvb_p.shape), vec(MLA_HP), vec(SWA_HD), vec(SWA_HD),
                  row(LANES), row(LANES), row(LANES), row(LANES)],
        out_specs=[row(w) for w, _ in outs],
        out_shape=[jax.ShapeDtypeStruct((N_TOK, w), dt) for w, dt in outs],
        compiler_params=_cparams(("arbitrary",)),
        name="mix_inputs",
    )(x, mod, mod, g1, w_in_p, gqa, w_qb_p, gq, gkv, w_kvb_p, gk, gsq, gsk, *tables)


def _cache_kv_kernel(ckv_ref, kr_ref, w_kvb_ref, gk_ref, km_ref, vm_ref):
    kr = kr_ref[...]
    kpe = jnp.concatenate([kr, jnp.zeros_like(kr)], axis=-1)
    _mla_keys(ckv_ref[...].astype(BF16), kpe, w_kvb_ref, gk_ref, None, km_ref, vm_ref)


def _cache_kv(layer, cache_ckv, cache_krope, w_kvb_p, gk):
    return pl.pallas_call(
        _cache_kv_kernel,
        grid=(DEC_BATCH,),
        in_specs=[pl.BlockSpec((None, None, PAST_LEN, MLA_KV_RANK), lambda b: (b, layer, 0, 0)),
                  pl.BlockSpec((None, None, PAST_LEN, MLA_ROPE), lambda b: (b, layer, 0, 0)),
                  _const_spec(w_kvb_p.shape), _const_spec((1, MLA_HP))],
        out_specs=[pl.BlockSpec((PAST_LEN, MLA_HEADS * MLA_HP), lambda b: (b, 0)),
                   pl.BlockSpec((PAST_LEN, MLA_WIDTH), lambda b: (b, 0))],
        out_shape=[jax.ShapeDtypeStruct((N_CACHE, MLA_HEADS * MLA_HP), BF16),
                   jax.ShapeDtypeStruct((N_CACHE, MLA_WIDTH), BF16)],
        compiler_params=_cparams(("arbitrary",)),
        name="cache_kv",
    )(cache_ckv, cache_krope, w_kvb_p, gk)


MLA_SCALE = MLA_QK ** -0.5
SWA_SCALE = SWA_HD ** -0.5
MLA_QB = 256


def _scores(q, k, scale):
    return lax.dot_general(q, k, _NT, preferred_element_type=F32) * scale


def _mla_ctx_kernel(q_ref, k_ref, v_ref, o_ref):
    s = _scores(q_ref[...], k_ref[...], MLA_SCALE)
    e = jnp.exp(s - jnp.max(s, axis=-1, keepdims=True))
    o = jnp.dot(e.astype(BF16), v_ref[...], preferred_element_type=F32)
    o_ref[...] = (o / jnp.sum(e, axis=-1, keepdims=True)).astype(BF16)


def _mla_lat_kernel(q_ref, k_ref, v_ref, kc_ref, vc_ref, o_ref):
    q = q_ref[...]
    s1 = _scores(q, k_ref[...], MLA_SCALE)
    s2 = _scores(q, kc_ref[...], MLA_SCALE)
    m = jnp.maximum(jnp.max(s1, axis=-1, keepdims=True), jnp.max(s2, axis=-1, keepdims=True))
    e1 = jnp.exp(s1 - m)
    e2 = jnp.exp(s2 - m)
    o = (jnp.dot(e1.astype(BF16), v_ref[...], preferred_element_type=F32)
         + jnp.dot(e2.astype(BF16), vc_ref[...], preferred_element_type=F32))
    d = jnp.sum(e1, axis=-1, keepdims=True) + jnp.sum(e2, axis=-1, keepdims=True)
    o_ref[...] = (o / d).astype(BF16)


def _mla_attention(qm, km, vm, kc, vc):
    ctx = pl.pallas_call(
        _mla_ctx_kernel,
        grid=(BATCH, MLA_HEADS),
        in_specs=[pl.BlockSpec((SEQ, MLA_HP), lambda b, h: (b, h)),
                  pl.BlockSpec((SEQ, MLA_HP), lambda b, h: (b, h)),
                  pl.BlockSpec((SEQ, MLA_V), lambda b, h: (b, h))],
        out_specs=pl.BlockSpec((SEQ, MLA_V), lambda b, h: (b, h)),
        out_shape=jax.ShapeDtypeStruct((N_CTX, MLA_WIDTH), BF16),
        compiler_params=_cparams(("arbitrary", "arbitrary")),
        name="mla_ctx",
    )(qm, km, vm)
    qpb = DEC_SEQ // MLA_QB
    q0 = N_CTX // MLA_QB
    s0 = N_CTX // DEC_SEQ
    lat = pl.pallas_call(
        _mla_lat_kernel,
        grid=(DEC_BATCH, MLA_HEADS, qpb),
        in_specs=[pl.BlockSpec((MLA_QB, MLA_HP), lambda b, h, n: (q0 + b * qpb + n, h)),
                  pl.BlockSpec((DEC_SEQ, MLA_HP), lambda b, h, n: (s0 + b, h)),
                  pl.BlockSpec((DEC_SEQ, MLA_V), lambda b, h, n: (s0 + b, h)),
                  pl.BlockSpec((PAST_LEN, MLA_HP), lambda b, h, n: (b, h)),
                  pl.BlockSpec((PAST_LEN, MLA_V), lambda b, h, n: (b, h))],
        out_specs=pl.BlockSpec((MLA_QB, MLA_V), lambda b, h, n: (b * qpb + n, h)),
        out_shape=jax.ShapeDtypeStruct((N_LAT, MLA_WIDTH), BF16),
        compiler_params=_cparams(("arbitrary", "arbitrary", "arbitrary")),
        name="mla_lat",
    )(qm, km, vm, kc, vc)
    return jnp.concatenate([ctx, lat], axis=0)


def _swa_ctx_kernel(sink_ref, q_ref, k_ref, v_ref, o_ref):
    hk = pl.program_id(1)
    k = k_ref[...]
    v = v_ref[...]
    for g in range(SWA_GROUP):
        sink = sink_ref[hk * SWA_GROUP + g]
        s = _scores(q_ref[:, g * SWA_HD:(g + 1) * SWA_HD], k, SWA_SCALE)
        m = jnp.maximum(jnp.max(s, axis=-1, keepdims=True), sink)
        e = jnp.exp(s - m)
        d = jnp.sum(e, axis=-1, keepdims=True) + jnp.exp(sink - m)
        o = jnp.dot(e.astype(BF16), v, preferred_element_type=F32)
        o_ref[:, g * SWA_HD:(g + 1) * SWA_HD] = (o / d).astype(BF16)


SWA_QB = 128
SWA_BAND = 3 * SWA_QB


def _swa_lat_kernel(sink_ref, q_ref, k_ref, v_ref, kc_ref, vc_ref, o_ref):
    hk = pl.program_id(1)
    n = pl.program_id(2)
    start = pl.multiple_of(jnp.clip((n - 1) * SWA_QB, 0, DEC_SEQ - SWA_BAND), SWA_QB)
    kw = k_ref[pl.ds(start, SWA_BAND), :]
    vw = v_ref[pl.ds(start, SWA_BAND), :]
    kc = kc_ref[...].astype(BF16)
    vc = vc_ref[...].astype(BF16)
    qpos = n * SWA_QB + lax.broadcasted_iota(jnp.int32, (SWA_QB, SWA_BAND), 0)
    kpos = start + lax.broadcasted_iota(jnp.int32, (SWA_QB, SWA_BAND), 1)
    in_win = jnp.abs(kpos - qpos) <= SWA_WIN
    for g in range(SWA_GROUP):
        sink = sink_ref[hk * SWA_GROUP + g]
        q = q_ref[:, g * SWA_HD:(g + 1) * SWA_HD]
        sb = jnp.where(in_win, _scores(q, kw, SWA_SCALE), NEG_INF)
        sc = _scores(q, kc, SWA_SCALE)
        m = jnp.maximum(jnp.maximum(jnp.max(sb, axis=-1, keepdims=True), jnp.max(sc, axis=-1, keepdims=True)),
                        sink)
        eb = jnp.exp(sb - m)
        ec = jnp.exp(sc - m)
        d = jnp.sum(eb, axis=-1, keepdims=True) + jnp.sum(ec, axis=-1, keepdims=True) + jnp.exp(sink - m)
        o = (jnp.dot(eb.astype(BF16), vw, preferred_element_type=F32)
             + jnp.dot(ec.astype(BF16), vc, preferred_element_type=F32))
        o_ref[:, g * SWA_HD:(g + 1) * SWA_HD] = (o / d).astype(BF16)


def _swa_attention(layer, sink, qs, ks, vs, cache_k, cache_v):
    gw = SWA_GROUP * SWA_HD
    smem = pl.BlockSpec(memory_space=pltpu.SMEM)
    ctx = pl.pallas_call(
        _swa_ctx_kernel,
        grid=(BATCH, SWA_KV_HEADS),
        in_specs=[smem,
                  pl.BlockSpec((SEQ, gw), lambda b, h: (b, h)),
                  pl.BlockSpec((SEQ, SWA_HD), lambda b, h: (b, h)),
                  pl.BlockSpec((SEQ, SWA_HD), lambda b, h: (b, h))],
        out_specs=pl.BlockSpec((SEQ, gw), lambda b, h: (b, h)),
        out_shape=jax.ShapeDtypeStruct((N_CTX, SWA_WIDTH), BF16),
        compiler_params=_cparams(("arbitrary", "arbitrary")),
        name="swa_ctx",
    )(sink, qs, ks, vs)
    qpb = DEC_SEQ // SWA_QB
    q0 = N_CTX // SWA_QB
    s0 = N_CTX // DEC_SEQ
    lat = pl.pallas_call(
        _swa_lat_kernel,
        grid=(DEC_BATCH, SWA_KV_HEADS, qpb),
        in_specs=[smem,
                  pl.BlockSpec((SWA_QB, gw), lambda b, h, n: (q0 + b * qpb + n, h)),
                  pl.BlockSpec((DEC_SEQ, SWA_HD), lambda b, h, n: (s0 + b, h)),
                  pl.BlockSpec((DEC_SEQ, SWA_HD), lambda b, h, n: (s0 + b, h)),
                  pl.BlockSpec((None, None, PAST_LEN, SWA_HD), lambda b, h, n: (b, layer, 0, h)),
                  pl.BlockSpec((None, None, PAST_LEN, SWA_HD), lambda b, h, n: (b, layer, 0, h))],
        out_specs=pl.BlockSpec((SWA_QB, gw), lambda b, h, n: (b * qpb + n, h)),
        out_shape=jax.ShapeDtypeStruct((N_LAT, SWA_WIDTH), BF16),
        compiler_params=_cparams(("arbitrary", "arbitrary", "arbitrary")),
        name="swa_lat",
    )(sink, qs, ks, vs, cache_k, cache_v)
    return jnp.concatenate([ctx, lat], axis=0)


CONV_HALO = 16
CONV_ROWS = 64


def _conv_kernel(seq_len, g_ref, w_ref, b_ref, lg_ref, lb_ref, o_ref, pad_ref, y_ref):
    zeros = jnp.zeros((CONV_HALO, CONV_CH), F32)
    pad_ref[0:CONV_HALO, :] = zeros
    pad_ref[CONV_HALO + seq_len:, :] = zeros
    pad_ref[CONV_HALO:CONV_HALO + seq_len, :] = g_ref[...]
    first_tap = CONV_HALO - CONV_W // 2
    sub = 8

    def chunk(rc, carry):
        base = pl.multiple_of(rc * CONV_ROWS, CONV_ROWS)
        for cb in range(CONV_CH // LANES):
            cols = slice(cb * LANES, (cb + 1) * LANES)
            acc = jnp.broadcast_to(b_ref[:, cols], (CONV_ROWS, LANES))
            for b in range(sub):
                z = None
                for k in range(b, CONV_W, sub):
                    term = w_ref[k:k + 1, cols] * pad_ref[pl.ds(base + (k - b), CONV_ROWS + sub), cols]
                    z = term if z is None else z + term
                acc = acc + z[first_tap + b:first_tap + b + CONV_ROWS]
            y_ref[pl.ds(base, CONV_ROWS), cols] = acc
        return carry

    lax.fori_loop(0, seq_len // CONV_ROWS, chunk, 0)
    y = y_ref[...]
    yc = y - jnp.mean(y, axis=-1, keepdims=True)
    yn = yc * lax.rsqrt(jnp.mean(yc * yc, axis=-1, keepdims=True) + EPS) * lg_ref[...] + lb_ref[...]
    o_ref[...] = (yn * jax.nn.sigmoid(yn)).astype(BF16)


def _conv_module(gg, w, b, lg, lb):
    outs = []
    for n_seq, seq_len, first in ((BATCH, SEQ, 0), (DEC_BATCH, DEC_SEQ, N_CTX // DEC_SEQ)):
        outs.append(pl.pallas_call(
            functools.partial(_conv_kernel, seq_len),
            grid=(n_seq,),
            in_specs=[pl.BlockSpec((seq_len, CONV_CH), lambda s, first=first: (first + s, 0)),
                      _const_spec((CONV_W, CONV_CH)), _const_spec((1, CONV_CH)),
                      _const_spec((1, CONV_CH)), _const_spec((1, CONV_CH))],
            out_specs=pl.BlockSpec((seq_len, CONV_CH), lambda s: (s, 0)),
            out_shape=jax.ShapeDtypeStruct((n_seq * seq_len, CONV_CH), BF16),
            scratch_shapes=[pltpu.VMEM((seq_len + 2 * CONV_HALO, CONV_CH), F32),
                            pltpu.VMEM((seq_len, CONV_CH), F32)],
            compiler_params=_cparams(("arbitrary",)),
            name="conv_module",
        )(gg, w, b, lg, lb))
    return jnp.concatenate(outs, axis=0)


OUT_TM = 256


def _out_kernel(om_ref, oc_ref, os_ref, x_ref, g1_ref, sh_ref, sc_ref, n2_ref, w_ref, x1_ref, h2t_ref):
    o = (jnp.dot(om_ref[...], w_ref[0:MLA_WIDTH, :], preferred_element_type=F32)
         + jnp.dot(oc_ref[...], w_ref[MLA_WIDTH:MLA_WIDTH + CONV_CH, :], preferred_element_type=F32)
         + jnp.dot(os_ref[...], w_ref[MLA_WIDTH + CONV_CH:, :], preferred_element_type=F32))
    x1 = x_ref[...] + g1_ref[...] * o
    x1_ref[...] = x1
    h2 = x1 * _rms(x1, D_MODEL) * n2_ref[...]
    h2 = h2 * (1.0 + sc_ref[...]) + sh_ref[...]
    h2t_ref[...] = h2.T.astype(BF16)


def _out_project(layer, o_mla, o_conv, o_swa, x, mod, norm2, w_out):
    tm = OUT_TM
    row = lambda w: pl.BlockSpec((tm, w), lambda i: (i, 0))
    return pl.pallas_call(
        _out_kernel,
        grid=(N_TOK // tm,),
        in_specs=[row(MLA_WIDTH), row(CONV_CH), row(SWA_WIDTH), row(D_MODEL),
                  _mod_spec(layer, 2, tm), _mod_spec(layer, 3, tm), _mod_spec(layer, 4, tm),
                  _const_spec((1, D_MODEL)), _const_spec(w_out.shape)],
        out_specs=[row(D_MODEL), pl.BlockSpec((D_MODEL, tm), lambda i: (0, i))],
        out_shape=[jax.ShapeDtypeStruct((N_TOK, D_MODEL), F32),
                   jax.ShapeDtypeStruct((D_MODEL, N_TOK), BF16)],
        compiler_params=_cparams(("arbitrary",)),
        name="out_project",
    )(o_mla, o_conv, o_swa, x, mod, mod, mod, norm2, w_out)


SEL_TQ = 256
CAND_LIMIT = tuple(PEER_TOPK // (r + 1) for r in range(8))


def _top16(s):
    rows = lax.broadcasted_iota(jnp.int32, s.shape, 0)
    r16 = lax.broadcasted_iota(jnp.int32, (PEER_TOPK, s.shape[1]), 0)
    rank = jnp.full(s.shape, float(PEER_TOPK), F32)
    top = jnp.zeros((PEER_TOPK, s.shape[1]), F32)
    for r in range(PEER_TOPK):
        m = jnp.max(s, axis=0, keepdims=True)
        idx = jnp.min(jnp.where(s == m, rows, PEER_NKEYS), axis=0, keepdims=True)
        hit = rows == idx
        rank = jnp.where(hit, float(r), rank)
        s = jnp.where(hit, BIG_NEG, s)
        top = jnp.where(r16 == r, m, top)
    return top, rank


def _joint_top16(a, b):
    c_lanes = a.shape[1]
    sub = lax.broadcasted_iota(jnp.int32, (8, c_lanes), 0)
    b_lo, b_hi = b[0:8], b[8:16]
    cand = [a[0:1] + b_lo, a[0:1] + b_hi]
    cidx = [sub, sub + 8]
    for r in range(1, 8):
        cand.append(jnp.where(sub < CAND_LIMIT[r], a[r:r + 1] + b_lo, BIG_NEG))
        cidx.append(sub + r * PEER_TOPK)
    cand.append(a[8:16] + b[0:1])
    cidx.append((sub + 8) * PEER_TOPK)
    sel = [jnp.zeros((8, c_lanes), F32) for _ in cand]
    for _ in range(PEER_TOPK):
        m = functools.reduce(jnp.maximum, cand)
        m = jnp.max(m, axis=0, keepdims=True)
        ci = functools.reduce(jnp.minimum, [jnp.where(cv == m, ix, 4 * PEER_TOPK * PEER_TOPK)
                                            for cv, ix in zip(cand, cidx)])
        ci = jnp.min(ci, axis=0, keepdims=True)
        for v in range(len(cand)):
            hit = cidx[v] == ci
            cand[v] = jnp.where(hit, BIG_NEG, cand[v])
            sel[v] = jnp.where(hit, 1.0, sel[v])
    return sel


def _select_kernel(h2t_ref, wqt_ref, keys_ref, r1_ref, f_ref, n_ref, e0_ref, qt_ref):
    qt_ref[...] = jnp.dot(wqt_ref[...], h2t_ref[...], preferred_element_type=F32).astype(BF16)

    def head(h, carry):
        for col in range(SEL_TQ // LANES):
            cs = slice(col * LANES, (col + 1) * LANES)
            q0 = qt_ref[pl.ds(pl.multiple_of(h * PEER_DQ, PEER_DQ), PEER_HALF), cs]
            q1 = qt_ref[pl.ds(pl.multiple_of(h * PEER_DQ + PEER_HALF, PEER_HALF), PEER_HALF), cs]
            s0 = jnp.dot(keys_ref[h, 0], q0, preferred_element_type=F32)
            s1 = jnp.dot(keys_ref[h, 1], q1, preferred_element_type=F32)
            a, rank0 = _top16(s0)
            b, rank1 = _top16(s1)
            sel = _joint_top16(a, b)
            ea = jnp.exp(a - a[0:1])
            eb = jnp.exp(b - b[0:1])
            counts = [jnp.sum(sel[0] + sel[1], axis=0, keepdims=True)]
            z = ea[0:1] * (sel[0] * eb[0:8] + sel[1] * eb[8:16])
            for r in range(1, 8):
                counts.append(jnp.sum(sel[r + 1], axis=0, keepdims=True))
                z = z + ea[r:r + 1] * (sel[r + 1] * eb[0:8])
            z = z + sel[9] * ea[8:16] * eb[0:1]
            zinv = 1.0 / jnp.sum(z, axis=0, keepdims=True)
            n_tab = jnp.zeros(rank0.shape, F32)
            for r in range(PEER_TOPK):
                n_r = counts[r] if r < 8 else sel[9][r - 8:r - 7]
                n_tab = jnp.where(rank0 == float(r), n_r, n_tab)
            r1_ref[h, :, cs] = rank1.astype(BF16)
            f_ref[h, :, cs] = jnp.exp(s1 - b[0:1]).astype(BF16)
            n_ref[h, :, cs] = n_tab
            e0_ref[h, :, cs] = jnp.exp(s0 - a[0:1]) * zinv
        return carry

    lax.fori_loop(0, PEER_HEADS, head, 0)


def _peer_select(h2t, wq_t, keys):
    tq = SEL_TQ
    tab = lambda: pl.BlockSpec((PEER_HEADS, PEER_NKEYS, tq), lambda i: (0, 0, i))
    shape = (PEER_HEADS, PEER_NKEYS, N_TOK)
    return pl.pallas_call(
        _select_kernel,
        grid=(N_TOK // tq,),
        in_specs=[pl.BlockSpec((D_MODEL, tq), lambda i: (0, i)), _const_spec(wq_t.shape),
                  _const_spec(keys.shape)],
        out_specs=[tab(), tab(), tab(), tab()],
        out_shape=[jax.ShapeDtypeStruct(shape, BF16), jax.ShapeDtypeStruct(shape, BF16),
                   jax.ShapeDtypeStruct(shape, F32), jax.ShapeDtypeStruct(shape, F32)],
        scratch_shapes=[pltpu.VMEM((PEER_HEADS * PEER_DQ, tq), BF16)],
        compiler_params=_cparams(("arbitrary",)),
        name="peer_select",
    )(h2t, wq_t, keys)


PEER_TM = 512
PEER_TE = 512


def _gelu(x):
    return 0.5 * x * (1.0 + lax.erf(x * (2.0 ** -0.5)))


def _peer_kernel(h2t_ref, u_ref, vt_ref, r1_ref, f_ref, n_ref, e0_ref, x1_ref, g2_ref, o_ref,
                 acc_ref, r1f_ref, ff_ref, w_ref):
    te = pl.program_id(1)

    @pl.when(te == 0)
    def _():
        acc_ref[...] = jnp.zeros_like(acc_ref)
        r1f_ref[...] = r1_ref[...].astype(F32)
        ff_ref[...] = f_ref[...].astype(F32)

    a_t = jnp.dot(u_ref[...], h2t_ref[...], preferred_element_type=F32)
    for ib in range(PEER_TE // PEER_NKEYS):
        i = te * (PEER_TE // PEER_NKEYS) + ib
        gate = jnp.zeros((PEER_NKEYS, PEER_TM), F32)
        for h in range(PEER_HEADS):
            n_row = n_ref[h, pl.ds(i, 1), :]
            e_row = e0_ref[h, pl.ds(i, 1), :]
            gate = gate + jnp.where(r1f_ref[h] < n_row, ff_ref[h], 0.0) * e_row
        a = a_t[ib * PEER_NKEYS:(ib + 1) * PEER_NKEYS, :]
        w_ref[ib * PEER_NKEYS:(ib + 1) * PEER_NKEYS, :] = (_gelu(a) * gate).astype(BF16)
    acc_ref[...] += jnp.dot(vt_ref[...], w_ref[...], preferred_element_type=F32)

    @pl.when(te == pl.num_programs(1) - 1)
    def _():
        o_ref[...] = x1_ref[...] + g2_ref[...] * acc_ref[...].T


def _peer(layer, h2t, u, vt, r1, f, n, e0, x1, mod):
    tm, te = PEER_TM, PEER_TE
    tab = lambda: pl.BlockSpec((PEER_HEADS, PEER_NKEYS, tm), lambda i, j: (0, 0, i))
    return pl.pallas_call(
        _peer_kernel,
        grid=(N_TOK // tm, PEER_EXPERTS // te),
        in_specs=[pl.BlockSpec((D_MODEL, tm), lambda i, j: (0, i)),
                  pl.BlockSpec((te, D_MODEL), lambda i, j: (j, 0)),
                  pl.BlockSpec((D_MODEL, te), lambda i, j: (0, j)),
                  tab(), tab(), tab(), tab(),
                  pl.BlockSpec((tm, D_MODEL), lambda i, j: (i, 0)),
                  _mod_spec(layer, 5, tm)],
        out_specs=pl.BlockSpec((tm, D_MODEL), lambda i, j: (i, 0)),
        out_shape=jax.ShapeDtypeStruct((N_TOK, D_MODEL), F32),
        scratch_shapes=[pltpu.VMEM((D_MODEL, tm), F32),
                        pltpu.VMEM((PEER_HEADS, PEER_NKEYS, tm), F32),
                        pltpu.VMEM((PEER_HEADS, PEER_NKEYS, tm), F32),
                        pltpu.VMEM((te, tm), BF16)],
        compiler_params=_cparams(("arbitrary", "arbitrary")),
        name="peer_dense",
    )(h2t, u, vt, r1, f, n, e0, x1, mod)


def _rope_tables(rot_dim):
    rows = jnp.repeat(jnp.arange(DEC_SEQ // GRID_W), GRID_W).astype(F32)
    cols = jnp.tile(jnp.arange(GRID_W), DEC_SEQ // GRID_W).astype(F32)
    a = rot_dim // 2
    freqs = ROPE_BASE ** (-jnp.arange(0, a, 2, dtype=F32) / a)
    ar, ac = rows[:, None] * freqs, cols[:, None] * freqs
    cos = jnp.concatenate([jnp.cos(ar), jnp.cos(ar), jnp.cos(ac), jnp.cos(ac)], axis=1)
    sin = jnp.concatenate([-jnp.sin(ar), jnp.sin(ar), -jnp.sin(ac), jnp.sin(ac)], axis=1)
    cos = jnp.pad(cos, ((0, 0), (0, LANES - rot_dim)), constant_values=1.0)
    sin = jnp.pad(sin, ((0, 0), (0, LANES - rot_dim)))
    cos = jnp.concatenate([jnp.ones((N_CTX, LANES), F32), jnp.tile(cos, (DEC_BATCH, 1))], axis=0)
    sin = jnp.concatenate([jnp.zeros((N_CTX, LANES), F32), jnp.tile(sin, (DEC_BATCH, 1))], axis=0)
    return cos, sin


def _pad_head_gain(g):
    return jnp.pad(g, (0, MLA_HP - MLA_QK)).reshape(1, MLA_HP)


def kernel(x_prompt, x_sample, cache_mla_ckv, cache_mla_krope, cache_swa_k, cache_swa_v, c, c_ctx, norm1_g,
           w_mod, b_mod, w_in, q_a_norm_g, w_q_b, kv_a_norm_g, w_kv_b, mla_q_norm_g, mla_k_norm_g, conv_dw_w,
           conv_dw_b, conv_ln_g, conv_ln_b, swa_q_norm_g, swa_k_norm_g, swa_sink, w_out, norm2_g,
           peer_w_q, peer_keys, peer_u, peer_v):
    x = jnp.concatenate([x_prompt.reshape(N_CTX, D_MODEL), x_sample.reshape(N_LAT, D_MODEL)], axis=0)
    cvec = jnp.concatenate([c_ctx[None, :], c, jnp.zeros((MOD_ROWS - 1 - DEC_BATCH, D_MODEL), F32)], axis=0)
    mod = _modulation(cvec, w_mod, b_mod).reshape(DEPTH, MOD_ROWS, 1, 6 * D_MODEL)
    tables = _rope_tables(MLA_ROPE) + _rope_tables(SWA_HD)
    cache_k = cache_swa_k.reshape(DEC_BATCH, DEPTH, PAST_LEN, SWA_KV_HEADS * SWA_HD)
    cache_v = cache_swa_v.reshape(DEC_BATCH, DEPTH, PAST_LEN, SWA_KV_HEADS * SWA_HD)

    ckv_l, krope_l, k_l, v_l = [], [], [], []
    for l in range(DEPTH):
        w_in_p = jnp.concatenate([w_in[l][:, :C_KR + MLA_ROPE], jnp.zeros((D_MODEL, MLA_ROPE), F32),
                                  w_in[l][:, C_KR + MLA_ROPE:]], axis=1).astype(BF16)
        w_qb_p = jnp.pad(w_q_b[l].reshape(MLA_Q_RANK, MLA_HEADS, MLA_QK),
                         ((0, 0), (0, 0), (0, MLA_HP - MLA_QK))).reshape(MLA_Q_RANK, MLA_HEADS * MLA_HP).astype(BF16)
        w_kvb3 = w_kv_b[l].reshape(MLA_KV_RANK, MLA_HEADS, MLA_NOPE + MLA_V)
        w_kvb_p = jnp.concatenate([w_kvb3[:, :, :MLA_NOPE].reshape(MLA_KV_RANK, MLA_HEADS * MLA_NOPE),
                                   w_kvb3[:, :, MLA_NOPE:].reshape(MLA_KV_RANK, MLA_WIDTH)], axis=1).astype(BF16)
        gk = _pad_head_gain(mla_k_norm_g[l])

        (qm, km, vm, gg, qs, ks, vs, ckv, krope, k2, v2) = _project(
            l, x, mod, norm1_g[l].reshape(1, D_MODEL), w_in_p, q_a_norm_g[l].reshape(1, MLA_Q_RANK), w_qb_p,
            _pad_head_gain(mla_q_norm_g[l]), kv_a_norm_g[l].reshape(1, MLA_KV_RANK), w_kvb_p, gk,
            swa_q_norm_g[l].reshape(1, SWA_HD), swa_k_norm_g[l].reshape(1, SWA_HD), tables)
        kc, vc = _cache_kv(l, cache_mla_ckv, cache_mla_krope, w_kvb_p, gk)
        o_mla = _mla_attention(qm, km, vm, kc, vc)
        o_swa = _swa_attention(l, swa_sink[l], qs, ks, vs, cache_k, cache_v)
        o_conv = _conv_module(gg, conv_dw_w[l], conv_dw_b[l].reshape(1, CONV_CH),
                              conv_ln_g[l].reshape(1, CONV_CH), conv_ln_b[l].reshape(1, CONV_CH))
        x1, h2t = _out_project(l, o_mla, o_conv, o_swa, x, mod, norm2_g[l].reshape(1, D_MODEL),
                               w_out[l].astype(BF16))
        r1, f, n, e0 = _peer_select(h2t, peer_w_q[l].T.astype(BF16), peer_keys[l].astype(BF16))
        x = _peer(l, h2t, peer_u[l].astype(BF16), peer_v[l].T.astype(BF16), r1, f, n, e0, x1, mod)

        ckv_l.append(ckv[:N_CTX].reshape(BATCH, SEQ, MLA_KV_RANK))
        krope_l.append(krope[:N_CTX].reshape(BATCH, SEQ, MLA_ROPE))
        k_l.append(k2[:N_CTX].reshape(BATCH, SEQ, SWA_KV_HEADS, SWA_HD))
        v_l.append(v2[:N_CTX].reshape(BATCH, SEQ, SWA_KV_HEADS, SWA_HD))

    return (x[:N_CTX].reshape(BATCH, SEQ, D_MODEL), x[N_CTX:].reshape(DEC_BATCH, DEC_SEQ, D_MODEL),
            jnp.stack(ckv_l, axis=1), jnp.stack(krope_l, axis=1), jnp.stack(k_l, axis=1), jnp.stack(v_l, axis=1))
```
